```python
import jax, jax.numpy as jnp
from jax import lax
import numpy as np

D_MODEL = 1024
BATCH = 8
SEQ = 2048
DEPTH = 1

D_FF = 4 * D_MODEL
GMLP_WIDTH = D_MODEL
GMLP_HEADS = 8
GMLP_HEAD_DIM = GMLP_WIDTH // GMLP_HEADS
CHUNK = 128
CONV_WIDTH = D_MODEL
CONV_KERNEL = 31
N_MOD = 9
EPS = 1e-6
SPLIT_SIZES = (GMLP_WIDTH, GMLP_WIDTH, CONV_WIDTH, CONV_WIDTH, D_MODEL, D_MODEL)
D_IN = sum(SPLIT_SIZES)

kernel_name = "hybrid_gmlp_conformer_conv_gated_macaron"


def _split_points(sizes):
    pts, acc = [], 0
    for s in sizes[:-1]:
        acc += s
        pts.append(acc)
    return pts


def rms_norm(x, g):
    xf = x.astype(jnp.float32)
    y = xf * lax.rsqrt(jnp.mean(xf * xf, axis=-1, keepdims=True) + EPS)
    return (y * g.astype(jnp.float32)).astype(x.dtype)


def layer_norm(x, g, b):
    xf = x.astype(jnp.float32)
    mu = jnp.mean(xf, axis=-1, keepdims=True)
    var = jnp.mean(jnp.square(xf - mu), axis=-1, keepdims=True)
    y = (xf - mu) * lax.rsqrt(var + EPS)
    return (y * g.astype(jnp.float32) + b.astype(jnp.float32)).astype(x.dtype)


def modulate(h, shift, scale):
    return h * (1.0 + scale[:, None, :]) + shift[:, None, :]


def swiglu(h, w_gate, w_up, w_down):
    return (jax.nn.silu(h @ w_gate) * (h @ w_up)) @ w_down


def token_mixer(h, w_in, b_in, sgu_ln_g, sgu_ln_b, sgu_w_s, sgu_b_s,
                conv_w, conv_b, conv_ln_g, conv_ln_b, w_branch_a, w_branch_b, w_out):
    bsz, seq, _ = h.shape
    proj = h @ w_in + b_in
    u, v, cv, cg, ga, gb = jnp.split(proj, _split_points(SPLIT_SIZES), axis=-1)

    u = jax.nn.gelu(u)
    v = layer_norm(jax.nn.gelu(v), sgu_ln_g, sgu_ln_b)
    n_chunks = seq // CHUNK
    v = v.reshape(bsz, n_chunks, CHUNK, GMLP_HEADS, GMLP_HEAD_DIM)
    causal = jnp.tril(jnp.ones((CHUNK, CHUNK), dtype=bool))
    w_s = jnp.where(causal[None], sgu_w_s, jnp.zeros_like(sgu_w_s))
    v_mix = jnp.einsum('hts,bcshd->bcthd', w_s, v) + sgu_b_s.T[None, None, :, :, None]
    y_a = (u * v_mix.reshape(bsz, seq, GMLP_WIDTH)) @ w_branch_a

    z = cv * jax.nn.sigmoid(cg)
    z = lax.conv_general_dilated(
        z, conv_w[:, None, :], window_strides=(1,),
        padding=[(CONV_KERNEL - 1, 0)],
        dimension_numbers=('NWC', 'WIO', 'NWC'),
        feature_group_count=CONV_WIDTH) + conv_b
    z = jax.nn.silu(layer_norm(z, conv_ln_g, conv_ln_b))
    y_b = z @ w_branch_b

    merged = jax.nn.sigmoid(ga) * y_a + jax.nn.sigmoid(gb) * y_b
    return merged @ w_out


def setup_inputs(seed: int = 0) -> dict:
    key = jax.random.key(seed)
    ks = jax.random.split(key, 32)
    f32 = jnp.float32

    def nrm(k, shape, fan_in, mult=1.0):
        return jax.random.normal(k, shape, f32) * (mult * fan_in ** -0.5)

    def gain(k, shape):
        return 1.0 + 0.05 * jax.random.normal(k, shape, f32)

    def bias(k, shape):
        return 0.02 * jax.random.normal(k, shape, f32)

    L = DEPTH
    return {
        "x": jax.random.normal(ks[0], (BATCH, SEQ, D_MODEL), f32),
        "c": jax.random.normal(ks[1], (BATCH, D_MODEL), f32),
        "ada_w": nrm(ks[2], (L, D_MODEL, N_MOD * D_MODEL), D_MODEL, 0.5),
        "ada_b": bias(ks[3], (L, N_MOD * D_MODEL)),
        "norm_ffn1": gain(ks[4], (L, D_MODEL)),
        "ffn1_w_gate": nrm(ks[5], (L, D_MODEL, D_FF), D_MODEL),
        "ffn1_w_up": nrm(ks[6], (L, D_MODEL, D_FF), D_MODEL),
        "ffn1_w_down": nrm(ks[7], (L, D_FF, D_MODEL), D_FF),
        "norm_mix": gain(ks[8], (L, D_MODEL)),
        "mix_w_in": nrm(ks[9], (L, D_MODEL, D_IN), D_MODEL),
        "mix_b_in": bias(ks[10], (L, D_IN)),
        "sgu_ln_g": gain(ks[11], (L, GMLP_WIDTH)),
        "sgu_ln_b": bias(ks[12], (L, GMLP_WIDTH)),
        "sgu_w_s": nrm(ks[13], (L, GMLP_HEADS, CHUNK, CHUNK), CHUNK),
        "sgu_b_s": gain(ks[14], (L, GMLP_HEADS, CHUNK)),
        "conv_w": nrm(ks[15], (L, CONV_KERNEL, CONV_WIDTH), CONV_KERNEL),
        "conv_b": bias(ks[16], (L, CONV_WIDTH)),
        "conv_ln_g": gain(ks[17], (L, CONV_WIDTH)),
        "conv_ln_b": bias(ks[18], (L, CONV_WIDTH)),
        "w_branch_a": nrm(ks[19], (L, GMLP_WIDTH, D_MODEL), GMLP_WIDTH),
        "w_branch_b": nrm(ks[20], (L, CONV_WIDTH, D_MODEL), CONV_WIDTH),
        "w_out": nrm(ks[21], (L, D_MODEL, D_MODEL), D_MODEL),
        "norm_ffn2": gain(ks[22], (L, D_MODEL)),
        "ffn2_w_gate": nrm(ks[23], (L, D_MODEL, D_FF), D_MODEL),
        "ffn2_w_up": nrm(ks[24], (L, D_MODEL, D_FF), D_MODEL),
        "ffn2_w_down": nrm(ks[25], (L, D_FF, D_MODEL), D_FF),
        "norm_final": gain(ks[26], (D_MODEL,)),
    }


def reference(x, c, ada_w, ada_b, norm_ffn1, ffn1_w_gate, ffn1_w_up, ffn1_w_down,
              norm_mix, mix_w_in, mix_b_in, sgu_ln_g, sgu_ln_b, sgu_w_s, sgu_b_s,
              conv_w, conv_b, conv_ln_g, conv_ln_b, w_branch_a, w_branch_b, w_out,
              norm_ffn2, ffn2_w_gate, ffn2_w_up, ffn2_w_down, norm_final):
    c_act = jax.nn.silu(c)
    for l in range(DEPTH):
        mod = c_act @ ada_w[l] + ada_b[l]
        sh1, sc1, g1, sh2, sc2, g2, sh3, sc3, g3 = jnp.split(mod, N_MOD, axis=-1)

        h = modulate(rms_norm(x, norm_ffn1[l]), sh1, sc1)
        x = x + 0.5 * g1[:, None, :] * swiglu(h, ffn1_w_gate[l], ffn1_w_up[l], ffn1_w_down[l])

        h = modulate(rms_norm(x, norm_mix[l]), sh2, sc2)
        y = token_mixer(h, mix_w_in[l], mix_b_in[l], sgu_ln_g[l], sgu_ln_b[l],
                        sgu_w_s[l], sgu_b_s[l], conv_w[l], conv_b[l], conv_ln_g[l],
                        conv_ln_b[l], w_branch_a[l], w_branch_b[l], w_out[l])
        x = x + g2[:, None, :] * y

        h = modulate(rms_norm(x, norm_ffn2[l]), sh3, sc3)
        x = x + 0.5 * g3[:, None, :] * swiglu(h, ffn2_w_gate[l], ffn2_w_up[l], ffn2_w_down[l])

    return rms_norm(x, norm_final)
```

```python
import functools

import jax
import jax.numpy as jnp
from jax import lax
from jax.experimental import pallas as pl
from jax.experimental.pallas import tpu as pltpu

EPS = 1e-6
CHUNK = 128
HEADS = 8
CONV_K = 31
HALO = 32
N_MOD = 9
LANES = 128
VMEM_LIMIT = 56 * 1024 * 1024

F32 = jnp.float32
BF16 = jnp.bfloat16


def _dot(a, b):
    return jnp.dot(a, b, preferred_element_type=F32)


def _rms(x, w):
    return x * lax.rsqrt(jnp.mean(x * x, axis=-1, keepdims=True) + EPS) * w


def _ln(x, g, b):
    mu = jnp.mean(x, axis=-1, keepdims=True)
    xc = x - mu
    var = jnp.mean(xc * xc, axis=-1, keepdims=True)
    return xc * lax.rsqrt(var + EPS) * g + b


def _resident(shape):
    nd = len(shape)
    return pl.BlockSpec(shape, lambda *_: (0,) * nd, pipeline_mode=pl.Buffered(1))


def _adaln_kernel(c_ref, w_ref, b_ref, o_ref):
    c = c_ref[...]
    ca = (c * jax.nn.sigmoid(c)).astype(BF16)
    o_ref[...] = _dot(ca, w_ref[...].astype(BF16)) + b_ref[...]


def _adaln(c, w, b):
    bsz, d = c.shape
    n = w.shape[1]
    tn = 1024
    return pl.pallas_call(
        _adaln_kernel,
        grid=(n // tn,),
        in_specs=[
            pl.BlockSpec((bsz, d), lambda j: (0, 0)),
            pl.BlockSpec((d, tn), lambda j: (0, j)),
            pl.BlockSpec((1, tn), lambda j: (0, j)),
        ],
        out_specs=pl.BlockSpec((bsz, tn), lambda j: (0, j)),
        out_shape=jax.ShapeDtypeStruct((bsz, n), F32),
        compiler_params=pltpu.CompilerParams(dimension_semantics=("arbitrary",)),
        name="adaln",
    )(c, w, b.reshape(1, n))


def _ffn_kernel(x_ref, mod_ref, nw_ref, wg_ref, wu_ref, wd_ref, fw_ref, o_ref, acc_ref,
                *, mod_base, n_chunks, fc, final_norm):
    x = x_ref[0]
    sh = mod_ref[0, mod_base:mod_base + 1, :]
    sc = mod_ref[0, mod_base + 1:mod_base + 2, :]
    gt = mod_ref[0, mod_base + 2:mod_base + 3, :]
    hb = (_rms(x, nw_ref[...]) * (1.0 + sc) + sh).astype(BF16)

    acc_ref[...] = jnp.zeros_like(acc_ref)

    def body(j, carry):
        g = _dot(hb, wg_ref[j])
        u = _dot(hb, wu_ref[j])
        a = (g * jax.nn.sigmoid(g) * u).astype(BF16)
        start = pl.multiple_of(j * fc, fc)
        acc_ref[...] += _dot(a, wd_ref[pl.ds(start, fc), :])
        return carry

    lax.fori_loop(0, n_chunks, body, 0)

    y = x + (0.5 * gt) * acc_ref[...]
    if final_norm:
        y = _rms(y, fw_ref[...])
    o_ref[0] = y


def _ffn(x, mod, norm_w, wg, wu, wd, final_w, *, mod_base, final_norm, tm=512, fc=512):
    bsz, seq, d = x.shape
    f = wd.shape[0]
    n_chunks = f // fc
    wg3 = wg.reshape(d, n_chunks, fc).transpose(1, 0, 2).astype(BF16)
    wu3 = wu.reshape(d, n_chunks, fc).transpose(1, 0, 2).astype(BF16)
    wdb = wd.astype(BF16)
    kern = functools.partial(_ffn_kernel, mod_base=mod_base, n_chunks=n_chunks, fc=fc,
                             final_norm=final_norm)
    return pl.pallas_call(
        kern,
        grid=(bsz, seq // tm),
        in_specs=[
            pl.BlockSpec((1, tm, d), lambda b, t: (b, t, 0)),
            pl.BlockSpec((1, N_MOD, d), lambda b, t: (b, 0, 0)),
            _resident((1, d)),
            _resident((n_chunks, d, fc)),
            _resident((n_chunks, d, fc)),
            _resident((f, d)),
            _resident((1, d)),
        ],
        out_specs=pl.BlockSpec((1, tm, d), lambda b, t: (b, t, 0)),
        out_shape=jax.ShapeDtypeStruct((bsz, seq, d), F32),
        scratch_shapes=[pltpu.VMEM((tm, d), F32)],
        compiler_params=pltpu.CompilerParams(
            dimension_semantics=("arbitrary", "arbitrary"),
            vmem_limit_bytes=VMEM_LIMIT),
        name="ffn_final" if final_norm else "ffn",
    )(x, mod, norm_w.reshape(1, d), wg3, wu3, wdb, final_w.reshape(1, d))


def _mixer_kernel(x_ref, mod_ref, nw_ref, win_ref, bin_ref, lng_ref, lnb_ref, ws_ref, bs_ref,
                  cw_ref, cb_ref, clg_ref, clb_ref, wa_ref, wb_ref, wo_ref, o_ref,
                  zbuf_ref, zc_ref, *, tm, d, conv_rows):
    t = pl.program_id(1)
    n_sub = tm // CHUNK
    x = x_ref[0]
    sh = mod_ref[0, 3:4, :]
    sc = mod_ref[0, 4:5, :]
    gt = mod_ref[0, 5:6, :]
    hb = (_rms(x, nw_ref[...]) * (1.0 + sc) + sh).astype(BF16)

    def proj(i):
        return _dot(hb, win_ref[:, i * d:(i + 1) * d]) + bin_ref[:, i * d:(i + 1) * d]

    u = jax.nn.gelu(proj(0))
    vb = _ln(jax.nn.gelu(proj(1)), lng_ref[...], lnb_ref[...]).astype(BF16)
    row = lax.broadcasted_iota(jnp.int32, (CHUNK, CHUNK), 0)
    col = lax.broadcasted_iota(jnp.int32, (CHUNK, CHUNK), 1)
    causal = row >= col
    head_out = []
    for h in range(HEADS):
        ws = jnp.where(causal, ws_ref[h], 0.0).astype(BF16)
        rhs = jnp.concatenate(
            [vb[c * CHUNK:(c + 1) * CHUNK, h * LANES:(h + 1) * LANES] for c in range(n_sub)],
            axis=1)
        head_out.append(_dot(ws, rhs) + bs_ref[:, h:h + 1])
    vmix = jnp.concatenate(
        [jnp.concatenate([head_out[h][:, c * LANES:(c + 1) * LANES] for h in range(HEADS)], axis=1)
         for c in range(n_sub)], axis=0)
    ya = _dot((u * vmix).astype(BF16), wa_ref[...])

    z = proj(2) * jax.nn.sigmoid(proj(3))

    @pl.when(t == 0)
    def _():
        zbuf_ref[0:HALO, :] = jnp.zeros((HALO, d), F32)

    zbuf_ref[HALO:HALO + tm, :] = z
    off = HALO - (CONV_K - 1)
    for r in range(tm // conv_rows):
        for cb in range(d // LANES):
            cs = slice(cb * LANES, (cb + 1) * LANES)
            acc = jnp.broadcast_to(cb_ref[:, cs], (conv_rows, LANES))
            for k in range(CONV_K):
                acc = acc + zbuf_ref[r * conv_rows + off + k:r * conv_rows + off + k + conv_rows, cs] * cw_ref[k:k + 1, cs]
            zc_ref[r * conv_rows:(r + 1) * conv_rows, cs] = acc
    zbuf_ref[0:HALO, :] = zbuf_ref[tm:tm + HALO, :]

    zl = _ln(zc_ref[...], clg_ref[...], clb_ref[...])
    yb = _dot((zl * jax.nn.sigmoid(zl)).astype(BF16), wb_ref[...])

    merged = jax.nn.sigmoid(proj(4)) * ya + jax.nn.sigmoid(proj(5)) * yb
    y = _dot(merged.astype(BF16), wo_ref[...])
    o_ref[0] = x + gt * y


def _mixer(x, mod, norm_w, w_in, b_in, ln_g, ln_b, w_s, b_s, conv_w, conv_b, cln_g, cln_b,
           w_a, w_b, w_o, *, tm=256, conv_rows=256):
    bsz, seq, d = x.shape
    d_in = w_in.shape[1]
    kern = functools.partial(_mixer_kernel, tm=tm, d=d, conv_rows=conv_rows)
    row = lambda a: a.reshape(1, -1)
    return pl.pallas_call(
        kern,
        grid=(bsz, seq // tm),
        in_specs=[
            pl.BlockSpec((1, tm, d), lambda b, t: (b, t, 0)),
            pl.BlockSpec((1, N_MOD, d), lambda b, t: (b, 0, 0)),
            _resident((1, d)),
            _resident((d, d_in)),
            _resident((1, d_in)),
            _resident((1, d)),
            _resident((1, d)),
            _resident((HEADS, CHUNK, CHUNK)),
            _resident((CHUNK, HEADS)),
            _resident((CONV_K, d)),
            _resident((1, d)),
            _resident((1, d)),
            _resident((1, d)),
            _resident((d, d)),
            _resident((d, d)),
            _resident((d, d)),
        ],
        out_specs=pl.BlockSpec((1, tm, d), lambda b, t: (b, t, 0)),
        out_shape=jax.ShapeDtypeStruct((bsz, seq, d), F32),
        scratch_shapes=[pltpu.VMEM((HALO + tm, d), F32), pltpu.VMEM((tm, d), F32)],
        compiler_params=pltpu.CompilerParams(
            dimension_semantics=("arbitrary", "arbitrary"),
            vmem_limit_bytes=VMEM_LIMIT),
        name="mixer",
    )(x, mod, row(norm_w), w_in.astype(BF16), row(b_in), row(ln_g), row(ln_b), w_s, b_s.T,
      conv_w, row(conv_b), row(cln_g), row(cln_b), w_a.astype(BF16), w_b.astype(BF16),
      w_o.astype(BF16))


def kernel(x, c, ada_w, ada_b, norm_ffn1, ffn1_w_gate, ffn1_w_up, ffn1_w_down, norm_mix, mix_w_in, mix_b_in, sgu_ln_g, sgu_ln_b, sgu_w_s, sgu_b_s, conv_w, conv_b, conv_ln_g, conv_ln_b, w_branch_a, w_branch_b, w_out, norm_ffn2, ffn2_w_gate, ffn2_w_up, ffn2_w_down, norm_final):
    bsz, seq, d = x.shape
    depth = ada_w.shape[0]
    for l in range(depth):
        mod = _adaln(c, ada_w[l], ada_b[l]).reshape(bsz, N_MOD, d)
        x = _ffn(x, mod, norm_ffn1[l], ffn1_w_gate[l], ffn1_w_up[l], ffn1_w_down[l], norm_final,
                 mod_base=0, final_norm=False)
        x = _mixer(x, mod, norm_mix[l], mix_w_in[l], mix_b_in[l], sgu_ln_g[l], sgu_ln_b[l],
                   sgu_w_s[l], sgu_b_s[l], conv_w[l], conv_b[l], conv_ln_g[l], conv_ln_b[l],
                   w_branch_a[l], w_branch_b[l], w_out[l])
        x = _ffn(x, mod, norm_ffn2[l], ffn2_w_gate[l], ffn2_w_up[l], ffn2_w_down[l], norm_final,
                 mod_base=6, final_norm=(l == depth - 1))
    return x
```

```python
import functools

import jax
import jax.numpy as jnp
from jax import lax
from jax.experimental import pallas as pl
from jax.experimental.pallas import tpu as pltpu

EPS = 1e-6
CHUNK = 128
HEADS = 8
CONV_K = 31
HALO = 32
N_MOD = 9
LANES = 128
PANEL = 512
VMEM_LIMIT = 56 * 1024 * 1024

F32 = jnp.float32
BF16 = jnp.bfloat16


def _dot(a, b):
    return jnp.dot(a, b, preferred_element_type=F32)


def _panels(w):
    k, n = w.shape
    return w.reshape(k, n // PANEL, PANEL).transpose(1, 0, 2).astype(BF16)


def _panel_dot(a, w_ref, first, count):
    return jnp.concatenate([_dot(a, w_ref[first + p]) for p in range(count)], axis=1)


def _rms(x, w):
    return x * lax.rsqrt(jnp.mean(x * x, axis=-1, keepdims=True) + EPS) * w


def _ln(x, g, b):
    mu = jnp.mean(x, axis=-1, keepdims=True)
    xc = x - mu
    var = jnp.mean(xc * xc, axis=-1, keepdims=True)
    return xc * lax.rsqrt(var + EPS) * g + b


def _resident(shape):
    nd = len(shape)
    return pl.BlockSpec(shape, lambda *_: (0,) * nd, pipeline_mode=pl.Buffered(1))


def _adaln_kernel(c_ref, w_ref, b_ref, o_ref):
    c = c_ref[...]
    ca = (c * jax.nn.sigmoid(c)).astype(BF16)
    o_ref[...] = _dot(ca, w_ref[...].astype(BF16)) + b_ref[...]


def _adaln(c, w, b):
    bsz, d = c.shape
    n = w.shape[1]
    tn = 1024
    return pl.pallas_call(
        _adaln_kernel,
        grid=(n // tn,),
        in_specs=[
            pl.BlockSpec((bsz, d), lambda j: (0, 0)),
            pl.BlockSpec((d, tn), lambda j: (0, j)),
            pl.BlockSpec((1, tn), lambda j: (0, j)),
        ],
        out_specs=pl.BlockSpec((bsz, tn), lambda j: (0, j)),
        out_shape=jax.ShapeDtypeStruct((bsz, n), F32),
        compiler_params=pltpu.CompilerParams(dimension_semantics=("arbitrary",)),
        name="adaln",
    )(c, w, b.reshape(1, n))


def _ffn_kernel(x_ref, mod_ref, nw_ref, wg_ref, wu_ref, wd_ref, fw_ref, o_ref, acc_ref, a_ref,
                *, mod_base, n_chunks, fc, final_norm):
    d = x_ref.shape[2]
    ppc = fc // PANEL
    x = x_ref[0]
    sh = mod_ref[0, mod_base:mod_base + 1, :]
    sc = mod_ref[0, mod_base + 1:mod_base + 2, :]
    gt = mod_ref[0, mod_base + 2:mod_base + 3, :]
    hb = (_rms(x, nw_ref[...]) * (1.0 + sc) + sh).astype(BF16)

    def act(j):
        g = _panel_dot(hb, wg_ref, j * ppc, ppc)
        u = _panel_dot(hb, wu_ref, j * ppc, ppc)
        return (g * jax.nn.sigmoid(g) * u).astype(BF16)

    def down(a, j):
        start = pl.multiple_of(j * fc, fc)
        return jnp.concatenate(
            [_dot(a, wd_ref[p, pl.ds(start, fc), :]) for p in range(d // PANEL)], axis=1)

    a_ref[...] = act(0)
    acc_ref[...] = jnp.zeros_like(acc_ref)

    def body(j, carry):
        a_prev = a_ref[...]
        a_ref[...] = act(j)
        acc_ref[...] += down(a_prev, j - 1)
        return carry

    lax.fori_loop(1, n_chunks, body, 0)

    y = x + (0.5 * gt) * (acc_ref[...] + down(a_ref[...], n_chunks - 1))
    if final_norm:
        y = _rms(y, fw_ref[...])
    o_ref[0] = y


def _ffn(x, mod, norm_w, wg, wu, wd, final_w, *, mod_base, final_norm, tm=512, fc=512):
    bsz, seq, d = x.shape
    f = wd.shape[0]
    n_chunks = f // fc
    kern = functools.partial(_ffn_kernel, mod_base=mod_base, n_chunks=n_chunks, fc=fc,
                             final_norm=final_norm)
    return pl.pallas_call(
        kern,
        grid=(bsz, seq // tm),
        in_specs=[
            pl.BlockSpec((1, tm, d), lambda b, t: (b, t, 0)),
            pl.BlockSpec((1, N_MOD, d), lambda b, t: (b, 0, 0)),
            _resident((1, d)),
            _resident((f // PANEL, d, PANEL)),
            _resident((f // PANEL, d, PANEL)),
            _resident((d // PANEL, f, PANEL)),
            _resident((1, d)),
        ],
        out_specs=pl.BlockSpec((1, tm, d), lambda b, t: (b, t, 0)),
        out_shape=jax.ShapeDtypeStruct((bsz, seq, d), F32),
        scratch_shapes=[pltpu.VMEM((tm, d), F32), pltpu.VMEM((tm, fc), BF16)],
        compiler_params=pltpu.CompilerParams(
            dimension_semantics=("arbitrary", "arbitrary"),
            vmem_limit_bytes=VMEM_LIMIT),
        name="ffn_final" if final_norm else "ffn",
    )(x, mod, norm_w.reshape(1, d), _panels(wg), _panels(wu), _panels(wd), final_w.reshape(1, d))


def _mixer_kernel(x_ref, mod_ref, nw_ref, win_ref, bin_ref, lng_ref, lnb_ref, ws_ref, bs_ref,
                  cw_ref, cb_ref, clg_ref, clb_ref, wa_ref, wb_ref, wo_ref, o_ref,
                  zbuf_ref, zc_ref, uv_ref, gate_ref, *, tm, d):
    n_sub = tm // CHUNK
    ppd = d // PANEL
    cpp = PANEL // LANES
    n_cb = d // LANES

    @pl.when(pl.program_id(1) == 0)
    def _():
        zbuf_ref[:, 0:HALO, :] = jnp.zeros((n_cb, HALO, LANES), F32)

    x = x_ref[0]
    sh = mod_ref[0, 3:4, :]
    sc = mod_ref[0, 4:5, :]
    gt = mod_ref[0, 5:6, :]
    hb = (_rms(x, nw_ref[...]) * (1.0 + sc) + sh).astype(BF16)

    def panel(p):
        return _dot(hb, win_ref[p]) + bin_ref[p]

    for p in range(ppd):
        z = panel(2 * ppd + p) * jax.nn.sigmoid(panel(3 * ppd + p))
        for c in range(cpp):
            zbuf_ref[p * cpp + c, HALO:HALO + tm, :] = z[:, c * LANES:(c + 1) * LANES]

    off = HALO - (CONV_K - 1)
    half = tm // 2

    def conv_block(cb):
        bias = jnp.broadcast_to(cb_ref[cb], (half, LANES))
        acc = [bias, bias]
        for k in range(CONV_K):
            wk = cw_ref[cb, k:k + 1, :]
            for q in range(2):
                acc[q] = acc[q] + zbuf_ref[cb, pl.ds(off + q + k, half, stride=2), :] * wk
        for q in range(2):
            zc_ref[cb, pl.ds(q, half, stride=2), :] = acc[q]
        zbuf_ref[cb, 0:HALO, :] = zbuf_ref[cb, tm:tm + HALO, :]

    def gelu_step(i, carry):
        conv_block(i)
        uv_ref[i] = jax.nn.gelu(panel(i))
        return carry

    def gate_step(i, carry):
        conv_block(2 * ppd + i)
        gate_ref[i] = jax.nn.sigmoid(panel(4 * ppd + i))
        return carry

    lax.fori_loop(0, 2 * ppd, gelu_step, 0)
    lax.fori_loop(0, 2 * ppd, gate_step, 0)

    u = jnp.concatenate([uv_ref[p] for p in range(ppd)], axis=1)
    v = jnp.concatenate([uv_ref[ppd + p] for p in range(ppd)], axis=1)
    vb = _ln(v, lng_ref[...], lnb_ref[...]).astype(BF16)
    row = lax.broadcasted_iota(jnp.int32, (CHUNK, CHUNK), 0)
    col = lax.broadcasted_iota(jnp.int32, (CHUNK, CHUNK), 1)
    causal = row >= col
    head_out = []
    for h in range(HEADS):
        ws = jnp.where(causal, ws_ref[h], 0.0).astype(BF16)
        rhs = jnp.concatenate(
            [vb[c * CHUNK:(c + 1) * CHUNK, h * LANES:(h + 1) * LANES] for c in range(n_sub)],
            axis=1)
        head_out.append(_dot(ws, rhs) + bs_ref[:, h:h + 1])
    vmix = jnp.concatenate(
        [jnp.concatenate([head_out[h][:, c * LANES:(c + 1) * LANES] for h in range(HEADS)], axis=1)
         for c in range(n_sub)], axis=0)
    ya = _panel_dot((u * vmix).astype(BF16), wa_ref, 0, ppd)

    zc = jnp.concatenate([zc_ref[cb] for cb in range(n_cb)], axis=1)
    zl = _ln(zc, clg_ref[...], clb_ref[...])
    yb = _panel_dot((zl * jax.nn.sigmoid(zl)).astype(BF16), wb_ref, 0, ppd)

    ga = jnp.concatenate([gate_ref[p] for p in range(ppd)], axis=1)
    gb = jnp.concatenate([gate_ref[ppd + p] for p in range(ppd)], axis=1)
    y = _panel_dot((ga * ya + gb * yb).astype(BF16), wo_ref, 0, ppd)
    o_ref[0] = x + gt * y


def _mixer(x, mod, norm_w, w_in, b_in, ln_g, ln_b, w_s, b_s, conv_w, conv_b, cln_g, cln_b,
           w_a, w_b, w_o, *, tm=256):
    bsz, seq, d = x.shape
    d_in = w_in.shape[1]
    n_cb = d // LANES
    kern = functools.partial(_mixer_kernel, tm=tm, d=d)
    row = lambda a: a.reshape(1, -1)
    cw3 = conv_w.reshape(CONV_K, n_cb, LANES).transpose(1, 0, 2)
    return pl.pallas_call(
        kern,
        grid=(bsz, seq // tm),
        in_specs=[
            pl.BlockSpec((1, tm, d), lambda b, t: (b, t, 0)),
            pl.BlockSpec((1, N_MOD, d), lambda b, t: (b, 0, 0)),
            _resident((1, d)),
            _resident((d_in // PANEL, d, PANEL)),
            _resident((d_in // PANEL, 1, PANEL)),
            _resident((1, d)),
            _resident((1, d)),
            _resident((HEADS, CHUNK, CHUNK)),
            _resident((CHUNK, HEADS)),
            _resident((n_cb, CONV_K, LANES)),
            _resident((n_cb, 1, LANES)),
            _resident((1, d)),
            _resident((1, d)),
            _resident((d // PANEL, d, PANEL)),
            _resident((d // PANEL, d, PANEL)),
            _resident((d // PANEL, d, PANEL)),
        ],
        out_specs=pl.BlockSpec((1, tm, d), lambda b, t: (b, t, 0)),
        out_shape=jax.ShapeDtypeStruct((bsz, seq, d), F32),
        scratch_shapes=[pltpu.VMEM((n_cb, HALO + tm, LANES), F32),
                        pltpu.VMEM((n_cb, tm, LANES), F32),
                        pltpu.VMEM((2 * d // PANEL, tm, PANEL), F32),
                        pltpu.VMEM((2 * d // PANEL, tm, PANEL), F32)],
        compiler_params=pltpu.CompilerParams(
            dimension_semantics=("arbitrary", "arbitrary"),
            vmem_limit_bytes=VMEM_LIMIT),
        name="mixer",
    )(x, mod, row(norm_w), _panels(w_in), b_in.reshape(d_in // PANEL, 1, PANEL), row(ln_g),
      row(ln_b), w_s, b_s.T, cw3, conv_b.reshape(n_cb, 1, LANES), row(cln_g), row(cln_b),
      _panels(w_a), _panels(w_b), _panels(w_o))


def kernel(x, c, ada_w, ada_b, norm_ffn1, ffn1_w_gate, ffn1_w_up, ffn1_w_down, norm_mix, mix_w_in, mix_b_in, sgu_ln_g, sgu_ln_b, sgu_w_s, sgu_b_s, conv_w, conv_b, conv_ln_g, conv_ln_b, w_branch_a, w_branch_b, w_out, norm_ffn2, ffn2_w_gate, ffn2_w_up, ffn2_w_down, norm_final):
    bsz, seq, d = x.shape
    depth = ada_w.shape[0]
    for l in range(depth):
        mod = _adaln(c, ada_w[l], ada_b[l]).reshape(bsz, N_MOD, d)
        x = _ffn(x, mod, norm_ffn1[l], ffn1_w_gate[l], ffn1_w_up[l], ffn1_w_down[l], norm_final,
                 mod_base=0, final_norm=False)
        x = _mixer(x, mod, norm_mix[l], mix_w_in[l], mix_b_in[l], sgu_ln_g[l], sgu_ln_b[l],
                   sgu_w_s[l], sgu_b_s[l], conv_w[l], conv_b[l], conv_ln_g[l], conv_ln_b[l],
                   w_branch_a[l], w_branch_b[l], w_out[l])
        x = _ffn(x, mod, norm_ffn2[l], ffn2_w_gate[l], ffn2_w_up[l], ffn2_w_down[l], norm_final,
                 mod_base=6, final_norm=(l == depth - 1))
    return x
```

```python
import functools

import jax
import jax.numpy as jnp
from jax import lax
from jax.experimental import pallas as pl
from jax.experimental.pallas import tpu as pltpu

EPS = 1e-6
CHUNK = 128
HEADS = 8
CONV_K = 31
HALO = 32
N_MOD = 9
LANES = 128
PANEL = 512
STAGE_ROWS = 1024
N_STAGE = 3
VMEM_LIMIT = 56 * 1024 * 1024

F32 = jnp.float32
BF16 = jnp.bfloat16


def _dot(a, b):
    return jnp.dot(a, b, preferred_element_type=F32)


def _panel_dot(a, w_ref, first, count):
    return jnp.concatenate([_dot(a, w_ref[first + p]) for p in range(count)], axis=1)


def _rms(x, w):
    return x * lax.rsqrt(jnp.mean(x * x, axis=-1, keepdims=True) + EPS) * w


def _ln(x, g, b):
    mu = jnp.mean(x, axis=-1, keepdims=True)
    xc = x - mu
    var = jnp.mean(xc * xc, axis=-1, keepdims=True)
    return xc * lax.rsqrt(var + EPS) * g + b


def _resident(shape):
    nd = len(shape)
    return pl.BlockSpec(shape, lambda *_: (0,) * nd, pipeline_mode=pl.Buffered(1))


_HBM = pl.BlockSpec(memory_space=pl.ANY)


def _panel_scratch(k, n):
    return pltpu.VMEM((n // PANEL, k, PANEL), BF16)


def _staging_scratch():
    return [pltpu.VMEM((N_STAGE, STAGE_ROWS, PANEL), F32), pltpu.SemaphoreType.DMA((N_STAGE,))]


def _stage_weights(plan, stage_ref, sem_ref):
    pieces = []
    for src, dst in plan:
        k, n = src.shape
        for p in range(n // PANEL):
            for r in range(k // STAGE_ROWS):
                pieces.append((src, dst, p, r))

    def dma(i):
        src, _, p, r = pieces[i]
        slot = i % N_STAGE
        return pltpu.make_async_copy(
            src.at[pl.ds(r * STAGE_ROWS, STAGE_ROWS), pl.ds(p * PANEL, PANEL)],
            stage_ref.at[slot], sem_ref.at[slot])

    for i in range(min(N_STAGE - 1, len(pieces))):
        dma(i).start()
    for i, (_, dst, p, r) in enumerate(pieces):
        if i + N_STAGE - 1 < len(pieces):
            dma(i + N_STAGE - 1).start()
        dma(i).wait()
        dst[p, r * STAGE_ROWS:(r + 1) * STAGE_ROWS, :] = stage_ref[i % N_STAGE].astype(BF16)


def _adaln_kernel(c_ref, w_ref, b_ref, o_ref):
    c = c_ref[...]
    ca = (c * jax.nn.sigmoid(c)).astype(BF16)
    o_ref[...] = _dot(ca, w_ref[...].astype(BF16)) + b_ref[...]


def _adaln(c, w, b):
    bsz, d = c.shape
    n = w.shape[1]
    tn = 1024
    return pl.pallas_call(
        _adaln_kernel,
        grid=(n // tn,),
        in_specs=[
            pl.BlockSpec((bsz, d), lambda j: (0, 0)),
            pl.BlockSpec((d, tn), lambda j: (0, j)),
            pl.BlockSpec((1, tn), lambda j: (0, j)),
        ],
        out_specs=pl.BlockSpec((bsz, tn), lambda j: (0, j)),
        out_shape=jax.ShapeDtypeStruct((bsz, n), F32),
        compiler_params=pltpu.CompilerParams(dimension_semantics=("arbitrary",)),
        name="adaln",
    )(c, w, b.reshape(1, n))


def _ffn_kernel(x_ref, mod_ref, nw_ref, wg_hbm, wu_hbm, wd_hbm, fw_ref, o_ref,
                wg_ref, wu_ref, wd_ref, stage_ref, sem_ref,
                *, mod_base, n_chunks, fc, final_norm):
    @pl.when((pl.program_id(0) == 0) & (pl.program_id(1) == 0))
    def _():
        _stage_weights([(wg_hbm, wg_ref), (wu_hbm, wu_ref), (wd_hbm, wd_ref)], stage_ref, sem_ref)

    d = x_ref.shape[2]
    ppc = fc // PANEL
    x = x_ref[0]
    sh = mod_ref[0, mod_base:mod_base + 1, :]
    sc = mod_ref[0, mod_base + 1:mod_base + 2, :]
    gt = mod_ref[0, mod_base + 2:mod_base + 3, :]
    hb = (_rms(x, nw_ref[...]) * (1.0 + sc) + sh).astype(BF16)

    def act(j):
        g = _panel_dot(hb, wg_ref, j * ppc, ppc)
        u = _panel_dot(hb, wu_ref, j * ppc, ppc)
        return (g * jax.nn.sigmoid(g) * u).astype(BF16)

    def down(a, j):
        return jnp.concatenate(
            [_dot(a, wd_ref[p, j * fc:(j + 1) * fc, :]) for p in range(d // PANEL)], axis=1)

    acc = down(act(0), 0)
    for j in range(1, n_chunks):
        acc = acc + down(act(j), j)

    y = x + (0.5 * gt) * acc
    if final_norm:
        y = _rms(y, fw_ref[...])
    o_ref[0] = y


def _ffn(x, mod, norm_w, wg, wu, wd, final_w, *, mod_base, final_norm, tm=512, fc=1024):
    bsz, seq, d = x.shape
    f = wd.shape[0]
    n_chunks = f // fc
    kern = functools.partial(_ffn_kernel, mod_base=mod_base, n_chunks=n_chunks, fc=fc,
                             final_norm=final_norm)
    return pl.pallas_call(
        kern,
        grid=(bsz, seq // tm),
        in_specs=[
            pl.BlockSpec((1, tm, d), lambda b, t: (b, t, 0)),
            pl.BlockSpec((1, N_MOD, d), lambda b, t: (b, 0, 0)),
            _resident((1, d)),
            _HBM, _HBM, _HBM,
            _resident((1, d)),
        ],
        out_specs=pl.BlockSpec((1, tm, d), lambda b, t: (b, t, 0)),
        out_shape=jax.ShapeDtypeStruct((bsz, seq, d), F32),
        scratch_shapes=[_panel_scratch(d, f), _panel_scratch(d, f), _panel_scratch(f, d)]
        + _staging_scratch(),
        compiler_params=pltpu.CompilerParams(
            dimension_semantics=("arbitrary", "arbitrary"),
            vmem_limit_bytes=VMEM_LIMIT),
        name="ffn_final" if final_norm else "ffn",
    )(x, mod, norm_w.reshape(1, d), wg, wu, wd, final_w.reshape(1, d))


def _mixer_kernel(x_ref, mod_ref, nw_ref, win_hbm, bin_ref, lng_ref, lnb_ref, ws_ref, bs_ref,
                  cw_ref, cb_ref, clg_ref, clb_ref, wa_hbm, wb_hbm, wo_hbm, o_ref,
                  win_ref, wa_ref, wb_ref, wo_ref, stage_ref, sem_ref,
                  zbuf_ref, zc_ref, uv_ref, gate_ref, *, tm, d):
    n_sub = tm // CHUNK
    ppd = d // PANEL
    cpp = PANEL // LANES
    n_cb = d // LANES

    @pl.when((pl.program_id(0) == 0) & (pl.program_id(1) == 0))
    def _():
        _stage_weights([(win_hbm, win_ref), (wa_hbm, wa_ref), (wb_hbm, wb_ref), (wo_hbm, wo_ref)],
                       stage_ref, sem_ref)

    @pl.when(pl.program_id(1) == 0)
    def _():
        zbuf_ref[:, 0:HALO, :] = jnp.zeros((n_cb, HALO, LANES), F32)

    x = x_ref[0]
    sh = mod_ref[0, 3:4, :]
    sc = mod_ref[0, 4:5, :]
    gt = mod_ref[0, 5:6, :]
    hb = (_rms(x, nw_ref[...]) * (1.0 + sc) + sh).astype(BF16)

    def panel(p):
        return _dot(hb, win_ref[p]) + bin_ref[p]

    for p in range(ppd):
        z = panel(2 * ppd + p) * jax.nn.sigmoid(panel(3 * ppd + p))
        for c in range(cpp):
            zbuf_ref[p * cpp + c, HALO:HALO + tm, :] = z[:, c * LANES:(c + 1) * LANES]

    off = HALO - (CONV_K - 1)
    half = tm // 2

    def conv_block(cb):
        bias = jnp.broadcast_to(cb_ref[cb], (half, LANES))
        acc = [bias, bias]
        for k in range(CONV_K):
            wk = cw_ref[cb, k:k + 1, :]
            for q in range(2):
                acc[q] = acc[q] + zbuf_ref[cb, pl.ds(off + q + k, half, stride=2), :] * wk
        for q in range(2):
            zc_ref[cb, pl.ds(q, half, stride=2), :] = acc[q]
        zbuf_ref[cb, 0:HALO, :] = zbuf_ref[cb, tm:tm + HALO, :]

    def gelu_step(i, carry):
        conv_block(i)
        uv_ref[i] = jax.nn.gelu(panel(i))
        return carry

    def gate_step(i, carry):
        conv_block(2 * ppd + i)
        gate_ref[i] = jax.nn.sigmoid(panel(4 * ppd + i))
        return carry

    lax.fori_loop(0, 2 * ppd, gelu_step, 0)
    lax.fori_loop(0, 2 * ppd, gate_step, 0)

    u = jnp.concatenate([uv_ref[p] for p in range(ppd)], axis=1)
    v = jnp.concatenate([uv_ref[ppd + p] for p in range(ppd)], axis=1)
    vb = _ln(v, lng_ref[...], lnb_ref[...]).astype(BF16)
    row = lax.broadcasted_iota(jnp.int32, (CHUNK, CHUNK), 0)
    col = lax.broadcasted_iota(jnp.int32, (CHUNK, CHUNK), 1)
    causal = row >= col
    head_out = []
    for h in range(HEADS):
        ws = jnp.where(causal, ws_ref[h], 0.0).astype(BF16)
        rhs = jnp.concatenate(
            [vb[c * CHUNK:(c + 1) * CHUNK, h * LANES:(h + 1) * LANES] for c in range(n_sub)],
            axis=1)
        head_out.append(_dot(ws, rhs) + bs_ref[:, h:h + 1])
    vmix = jnp.concatenate(
        [jnp.concatenate([head_out[h][:, c * LANES:(c + 1) * LANES] for h in range(HEADS)], axis=1)
         for c in range(n_sub)], axis=0)
    ya = _panel_dot((u * vmix).astype(BF16), wa_ref, 0, ppd)

    zc = jnp.concatenate([zc_ref[cb] for cb in range(n_cb)], axis=1)
    zl = _ln(zc, clg_ref[...], clb_ref[...])
    yb = _panel_dot((zl * jax.nn.sigmoid(zl)).astype(BF16), wb_ref, 0, ppd)

    ga = jnp.concatenate([gate_ref[p] for p in range(ppd)], axis=1)
    gb = jnp.concatenate([gate_ref[ppd + p] for p in range(ppd)], axis=1)
    y = _panel_dot((ga * ya + gb * yb).astype(BF16), wo_ref, 0, ppd)
    o_ref[0] = x + gt * y


def _mixer(x, mod, norm_w, w_in, b_in, ln_g, ln_b, w_s, b_s, conv_w, conv_b, cln_g, cln_b,
           w_a, w_b, w_o, *, tm=256):
    bsz, seq, d = x.shape
    d_in = w_in.shape[1]
    n_cb = d // LANES
    kern = functools.partial(_mixer_kernel, tm=tm, d=d)
    row = lambda a: a.reshape(1, -1)
    cw3 = conv_w.reshape(CONV_K, n_cb, LANES).transpose(1, 0, 2)
    return pl.pallas_call(
        kern,
        grid=(bsz, seq // tm),
        in_specs=[
            pl.BlockSpec((1, tm, d), lambda b, t: (b, t, 0)),
            pl.BlockSpec((1, N_MOD, d), lambda b, t: (b, 0, 0)),
            _resident((1, d)),
            _HBM,
            _resident((d_in // PANEL, 1, PANEL)),
            _resident((1, d)),
            _resident((1, d)),
            _resident((HEADS, CHUNK, CHUNK)),
            _resident((CHUNK, HEADS)),
            _resident((n_cb, CONV_K, LANES)),
            _resident((n_cb, 1, LANES)),
            _resident((1, d)),
            _resident((1, d)),
            _HBM, _HBM, _HBM,
        ],
        out_specs=pl.BlockSpec((1, tm, d), lambda b, t: (b, t, 0)),
        out_shape=jax.ShapeDtypeStruct((bsz, seq, d), F32),
        scratch_shapes=[_panel_scratch(d, d_in), _panel_scratch(d, d), _panel_scratch(d, d),
                        _panel_scratch(d, d)] + _staging_scratch() + [
            pltpu.VMEM((n_cb, HALO + tm, LANES), F32),
            pltpu.VMEM((n_cb, tm, LANES), F32),
            pltpu.VMEM((2 * d // PANEL, tm, PANEL), F32),
            pltpu.VMEM((2 * d // PANEL, tm, PANEL), F32)],
        compiler_params=pltpu.CompilerParams(
            dimension_semantics=("arbitrary", "arbitrary"),
            vmem_limit_bytes=VMEM_LIMIT),
        name="mixer",
    )(x, mod, row(norm_w), w_in, b_in.reshape(d_in // PANEL, 1, PANEL), row(ln_g),
      row(ln_b), w_s, b_s.T, cw3, conv_b.reshape(n_cb, 1, LANES), row(cln_g), row(cln_b),
      w_a, w_b, w_o)


def kernel(x, c, ada_w, ada_b, norm_ffn1, ffn1_w_gate, ffn1_w_up, ffn1_w_down, norm_mix, mix_w_in, mix_b_in, sgu_ln_g, sgu_ln_b, sgu_w_s, sgu_b_s, conv_w, conv_b, conv_ln_g, conv_ln_b, w_branch_a, w_branch_b, w_out, norm_ffn2, ffn2_w_gate, ffn2_w_up, ffn2_w_down, norm_final):
    bsz, seq, d = x.shape
    depth = ada_w.shape[0]
    for l in range(depth):
        mod = _adaln(c, ada_w[l], ada_b[l]).reshape(bsz, N_MOD, d)
        x = _ffn(x, mod, norm_ffn1[l], ffn1_w_gate[l], ffn1_w_up[l], ffn1_w_down[l], norm_final,
                 mod_base=0, final_norm=False)
        x = _mixer(x, mod, norm_mix[l], mix_w_in[l], mix_b_in[l], sgu_ln_g[l], sgu_ln_b[l],
                   sgu_w_s[l], sgu_b_s[l], conv_w[l], conv_b[l], conv_ln_g[l], conv_ln_b[l],
                   w_branch_a[l], w_branch_b[l], w_out[l])
        x = _ffn(x, mod, norm_ffn2[l], ffn2_w_gate[l], ffn2_w_up[l], ffn2_w_down[l], norm_final,
                 mod_base=6, final_norm=(l == depth - 1))
    return x
```
